```python
import math
import jax, jax.numpy as jnp
from jax import lax
import numpy as np

D_MODEL = 2048
BATCH = 1
SEQ = 8192
DEPTH = 1
DEC_BATCH = 8
DEC_SEQ = 4096
PAST_LEN = 128

GRID_W = 64
NA_HEAD_DIM = 128
S5_WIDTH = D_MODEL // 4
NA_WIDTH = D_MODEL - S5_WIDTH
NA_HEADS = NA_WIDTH // NA_HEAD_DIM
WIN_R = 8
WIN_C = 16
COL_QBLK = 16
COL_KBLK = 32
S5_GROUP = 16
S5_GROUPS = S5_WIDTH // S5_GROUP
S5_STATE = 64
DT_MIN = 1e-3
DT_MAX = 1e-1
N_EGROUPS = 4
EXP_PER_GROUP = 8
N_EXPERTS = N_EGROUPS * EXP_PER_GROUP
TOP_K = 2
EXPERT_FF = D_MODEL // 2
MOE_BLOCK = 256
EPS = 1e-6
IN_WIDTH = 3 * NA_WIDTH + S5_WIDTH + 2 * D_MODEL

kernel_name = 'hybrid_natten_s5_hmoe_encoder'


def _rmsnorm(x, g):
    xf = x.astype(jnp.float32)
    y = xf * lax.rsqrt(jnp.mean(xf * xf, axis=-1, keepdims=True) + EPS) * g.astype(jnp.float32)
    return y.astype(x.dtype)


def _col_tables():
    nb = GRID_W // COL_QBLK
    qc = np.arange(GRID_W).reshape(nb, COL_QBLK)
    kstart = np.clip(np.arange(nb) * COL_QBLK - WIN_C // 2, 0, GRID_W - COL_KBLK)
    kc = kstart[:, None] + np.arange(COL_KBLK)[None, :]
    wstart = np.clip(qc - WIN_C // 2, 0, GRID_W - WIN_C)
    valid = (kc[:, None, :] >= wstart[:, :, None]) & (kc[:, None, :] < wstart[:, :, None] + WIN_C)
    dc_idx = np.clip(kc[:, None, :] - qc[:, :, None] + WIN_C - 1, 0, 2 * WIN_C - 2)
    return kc, valid, dc_idx


def _neighbourhood_attention(q, k, v, rpb):
    b, t, h, dh = q.shape
    rows = t // GRID_W
    kr = min(WIN_R, rows)
    nb = GRID_W // COL_QBLK
    kc, valid, dc_idx = _col_tables()
    kg = k.reshape(b, rows, GRID_W, h, dh)
    vg = v.reshape(b, rows, GRID_W, h, dh)
    qg = q.reshape(b, rows, nb, COL_QBLK, h, dh).transpose(1, 0, 2, 3, 4, 5)
    bias_c = rpb.astype(jnp.float32)[:, :, dc_idx]
    mask = jnp.asarray(valid)[:, :, None, :]
    scale = dh ** -0.5

    def row_step(args):
        r, q_row = args
        rs = jnp.clip(r - kr // 2, 0, rows - kr)
        k_rows = lax.dynamic_slice_in_dim(kg, rs, kr, axis=1)
        v_rows = lax.dynamic_slice_in_dim(vg, rs, kr, axis=1)
        k_blk = k_rows[:, :, kc]
        v_blk = v_rows[:, :, kc]
        s = jnp.einsum('bnqhd,binjhd->bhnqij', q_row, k_blk).astype(jnp.float32) * scale
        dr_idx = rs + jnp.arange(kr, dtype=jnp.int32) - r + WIN_R - 1
        bias = jnp.take(bias_c, dr_idx, axis=1).transpose(0, 2, 3, 1, 4)
        s = jnp.where(mask, s + bias[None], -jnp.inf)
        p = jax.nn.softmax(s.reshape(b, h, nb, COL_QBLK, kr * COL_KBLK), axis=-1)
        p = p.reshape(b, h, nb, COL_QBLK, kr, COL_KBLK).astype(v.dtype)
        return jnp.einsum('bhnqij,binjhd->bnqhd', p, v_blk)

    out = lax.map(row_step, (jnp.arange(rows, dtype=jnp.int32), qg))
    return out.transpose(1, 0, 2, 3, 4, 5).reshape(b, t, h * dh)


def _s5_direction(u_g, lam_re, lam_im, log_step, b_re, b_im, c_re, c_im, reverse):
    l = u_g.shape[1]
    lam = lax.complex(lam_re.astype(jnp.float32), lam_im.astype(jnp.float32))
    step = jnp.exp(log_step.astype(jnp.float32))[:, None]
    lam_bar = jnp.exp(lam * step)
    b_bar = ((lam_bar - 1.0) / lam)[..., None] * lax.complex(b_re.astype(jnp.float32), b_im.astype(jnp.float32))
    bu = lax.complex(jnp.einsum('gph,blgh->blgp', jnp.real(b_bar), u_g),
                     jnp.einsum('gph,blgh->blgp', jnp.imag(b_bar), u_g))
    a = jnp.broadcast_to(lam_bar, (1, l) + lam_bar.shape)

    def combine(e1, e2):
        a1, b1 = e1
        a2, b2 = e2
        return a1 * a2, a2 * b1 + b2

    _, xs = lax.associative_scan(combine, (a, bu), axis=1, reverse=reverse)
    return (jnp.einsum('ghp,blgp->blgh', c_re.astype(jnp.float32), jnp.real(xs))
            - jnp.einsum('ghp,blgp->blgh', c_im.astype(jnp.float32), jnp.imag(xs)))


def _s5_glu(u, lam_re, lam_im, log_step, b_re, b_im, c_re, c_im, d_skip, w_glu):
    b, l, _ = u.shape
    u_g = u.astype(jnp.float32).reshape(b, l, S5_GROUPS, S5_GROUP)
    y = _s5_direction(u_g, lam_re[0], lam_im[0], log_step[0], b_re[0], b_im[0], c_re[0], c_im[0], False)
    y = y + _s5_direction(u_g, lam_re[1], lam_im[1], log_step[1], b_re[1], b_im[1], c_re[1], c_im[1], True)
    y = y + d_skip.astype(jnp.float32).reshape(S5_GROUPS, S5_GROUP) * u_g
    y = jax.nn.gelu(y.reshape(b, l, S5_WIDTH)).astype(u.dtype)
    gl = y @ w_glu
    return gl[..., :S5_WIDTH] * jax.nn.sigmoid(gl[..., S5_WIDTH:])


def _hier_moe(h, w_route_g, b_route_g, w_route_e, b_route_e, w_gate, w_up, w_down):
    n, d = h.shape
    hf = h.astype(jnp.float32)
    g_prob = jax.nn.softmax(hf @ w_route_g.astype(jnp.float32) + b_route_g.astype(jnp.float32), axis=-1)
    g_w, g_idx = lax.top_k(g_prob, 1)
    e_logits = (hf @ w_route_e.astype(jnp.float32) + b_route_e.astype(jnp.float32)).reshape(n, N_EGROUPS, EXP_PER_GROUP)
    e_sel = jnp.take_along_axis(e_logits, g_idx[:, :, None], axis=1)[:, 0]
    e_w, e_idx = lax.top_k(jax.nn.softmax(e_sel, axis=-1), TOP_K)
    gates = g_w * (e_w / jnp.sum(e_w, axis=-1, keepdims=True))
    expert = g_idx * EXP_PER_GROUP + e_idx

    m = n * TOP_K
    eid = expert.reshape(m)
    tok = jnp.arange(m, dtype=jnp.int32) // TOP_K
    gate = gates.reshape(m)
    order = jnp.argsort(eid)
    s_eid, s_tok, s_gate = eid[order], tok[order], gate[order]
    counts = jnp.bincount(eid, length=N_EXPERTS)
    starts = jnp.cumsum(counts) - counts
    padded = (counts + MOE_BLOCK - 1) // MOE_BLOCK * MOE_BLOCK
    p_ends = jnp.cumsum(padded)
    p_starts = p_ends - padded
    dest = p_starts[s_eid] + jnp.arange(m, dtype=jnp.int32) - starts[s_eid]
    n_blocks = -(-m // MOE_BLOCK) + N_EXPERTS
    buf_tok = jnp.full((n_blocks * MOE_BLOCK,), n, jnp.int32).at[dest].set(s_tok)
    h_pad = jnp.concatenate([h, jnp.zeros((1, d), h.dtype)], axis=0)
    x_buf = h_pad[buf_tok].reshape(n_blocks, MOE_BLOCK, d)
    blk_start = jnp.arange(n_blocks, dtype=jnp.int32) * MOE_BLOCK
    blk_eid = jnp.minimum(jnp.sum(p_ends[None, :] <= blk_start[:, None], axis=1), N_EXPERTS - 1)

    def expert_block(args):
        xb, e = args
        hid = jax.nn.silu(xb @ w_gate[e]) * (xb @ w_up[e])
        return hid @ w_down[e]

    y_buf = lax.map(expert_block, (x_buf, blk_eid)).reshape(n_blocks * MOE_BLOCK, d)
    y_assign = y_buf[dest] * s_gate[:, None].astype(h.dtype)
    return jax.ops.segment_sum(y_assign, s_tok, num_segments=n)


def _layer(x, n1, win, rpb_l, lam_re_l, lam_im_l, log_step_l, b_re_l, b_im_l, c_re_l, c_im_l, d_skip_l,
           w_glu_l, w_pa, w_pb, w_o, n2, w_rg, b_rg, w_re, b_re_r, w_g, w_u, w_d):
    b, t, d = x.shape
    h = _rmsnorm(x, n1)
    z = h @ win
    o1, o2, o3 = NA_WIDTH, 2 * NA_WIDTH, 3 * NA_WIDTH
    o4 = o3 + S5_WIDTH
    o5 = o4 + D_MODEL
    q = z[..., :o1].reshape(b, t, NA_HEADS, NA_HEAD_DIM)
    k = z[..., o1:o2].reshape(b, t, NA_HEADS, NA_HEAD_DIM)
    v = z[..., o2:o3].reshape(b, t, NA_HEADS, NA_HEAD_DIM)
    u = z[..., o3:o4]
    g_a = z[..., o4:o5]
    g_b = z[..., o5:]
    y_a = _neighbourhood_attention(q, k, v, rpb_l) @ w_pa
    y_b = _s5_glu(u, lam_re_l, lam_im_l, log_step_l, b_re_l, b_im_l, c_re_l, c_im_l, d_skip_l, w_glu_l) @ w_pb
    mixed = jax.nn.sigmoid(g_a) * y_a + jax.nn.sigmoid(g_b) * y_b
    x = x + mixed @ w_o
    h2 = _rmsnorm(x, n2).reshape(b * t, d)
    return x + _hier_moe(h2, w_rg, b_rg, w_re, b_re_r, w_g, w_u, w_d).reshape(b, t, d)


def _trunk(x, norm1_g, w_in, rpb, lam_re, lam_im, log_step, b_re, b_im, c_re, c_im, d_skip, w_glu,
           w_branch_a, w_branch_b, w_out, norm2_g, w_route_g, b_route_g, w_route_e, b_route_e,
           w_gate, w_up, w_down, final_g):
    for l in range(DEPTH):
        x = _layer(x, norm1_g[l], w_in[l], rpb[l], lam_re[l], lam_im[l], log_step[l], b_re[l], b_im[l],
                   c_re[l], c_im[l], d_skip[l], w_glu[l], w_branch_a[l], w_branch_b[l], w_out[l],
                   norm2_g[l], w_route_g[l], b_route_g[l], w_route_e[l], b_route_e[l],
                   w_gate[l], w_up[l], w_down[l])
    return _rmsnorm(x, final_g)


def setup_inputs(seed: int = 0) -> dict:
    key = jax.random.key(seed)
    ks = jax.random.split(key, 32)
    f32 = jnp.float32
    L, G, P, Hg = DEPTH, S5_GROUPS, S5_STATE, S5_GROUP

    def nrm(k, shape, scale):
        return jax.random.normal(k, shape, f32) * scale

    n_idx = jnp.arange(P, dtype=f32)
    return {
        'x_prompt': nrm(ks[0], (BATCH, SEQ, D_MODEL), 1.0),
        'x_sample': nrm(ks[1], (DEC_BATCH, DEC_SEQ, D_MODEL), 1.0),
        'norm1_g': 1.0 + nrm(ks[2], (L, D_MODEL), 0.02),
        'w_in': nrm(ks[3], (L, D_MODEL, IN_WIDTH), D_MODEL ** -0.5),
        'rpb': nrm(ks[4], (L, NA_HEADS, 2 * WIN_R - 1, 2 * WIN_C - 1), 0.1),
        'lam_re': -0.5 + nrm(ks[5], (L, 2, G, P), 0.01),
        'lam_im': math.pi * n_idx + nrm(ks[6], (L, 2, G, P), 0.01),
        'log_step': jax.random.uniform(ks[7], (L, 2, G), f32, math.log(DT_MIN), math.log(DT_MAX)),
        'b_re': nrm(ks[8], (L, 2, G, P, Hg), (2 * Hg) ** -0.5),
        'b_im': nrm(ks[9], (L, 2, G, P, Hg), (2 * Hg) ** -0.5),
        'c_re': nrm(ks[10], (L, 2, G, Hg, P), P ** -0.5),
        'c_im': nrm(ks[11], (L, 2, G, Hg, P), P ** -0.5),
        'd_skip': nrm(ks[12], (L, S5_WIDTH), 1.0),
        'w_glu': nrm(ks[13], (L, S5_WIDTH, 2 * S5_WIDTH), S5_WIDTH ** -0.5),
        'w_branch_a': nrm(ks[14], (L, NA_WIDTH, D_MODEL), NA_WIDTH ** -0.5),
        'w_branch_b': nrm(ks[15], (L, S5_WIDTH, D_MODEL), S5_WIDTH ** -0.5),
        'w_out': nrm(ks[16], (L, D_MODEL, D_MODEL), D_MODEL ** -0.5),
        'norm2_g': 1.0 + nrm(ks[17], (L, D_MODEL), 0.02),
        'w_route_g': nrm(ks[18], (L, D_MODEL, N_EGROUPS), D_MODEL ** -0.5),
        'b_route_g': nrm(ks[19], (L, N_EGROUPS), 0.01),
        'w_route_e': nrm(ks[20], (L, D_MODEL, N_EXPERTS), D_MODEL ** -0.5),
        'b_route_e': nrm(ks[21], (L, N_EXPERTS), 0.01),
        'w_gate': nrm(ks[22], (L, N_EXPERTS, D_MODEL, EXPERT_FF), D_MODEL ** -0.5),
        'w_up': nrm(ks[23], (L, N_EXPERTS, D_MODEL, EXPERT_FF), D_MODEL ** -0.5),
        'w_down': nrm(ks[24], (L, N_EXPERTS, EXPERT_FF, D_MODEL), EXPERT_FF ** -0.5),
        'final_g': 1.0 + nrm(ks[25], (D_MODEL,), 0.02),
    }


def reference(x_prompt, x_sample, norm1_g, w_in, rpb, lam_re, lam_im, log_step, b_re, b_im, c_re, c_im,
              d_skip, w_glu, w_branch_a, w_branch_b, w_out, norm2_g, w_route_g, b_route_g, w_route_e,
              b_route_e, w_gate, w_up, w_down, final_g):
    y_prompt = _trunk(x_prompt, norm1_g, w_in, rpb, lam_re, lam_im, log_step, b_re, b_im, c_re, c_im,
                      d_skip, w_glu, w_branch_a, w_branch_b, w_out, norm2_g, w_route_g, b_route_g,
                      w_route_e, b_route_e, w_gate, w_up, w_down, final_g)
    y_sample = _trunk(x_sample, norm1_g, w_in, rpb, lam_re, lam_im, log_step, b_re, b_im, c_re, c_im,
                      d_skip, w_glu, w_branch_a, w_branch_b, w_out, norm2_g, w_route_g, b_route_g,
                      w_route_e, b_route_e, w_gate, w_up, w_down, final_g)
    return (y_prompt, y_sample)
```

```python
import functools

import numpy as np
import jax
import jax.numpy as jnp
from jax import lax
from jax.experimental import pallas as pl
from jax.experimental.pallas import tpu as pltpu

F32 = jnp.float32
BF16 = jnp.bfloat16

GRID_W = 64
NA_HEAD_DIM = 128
TOP_K = 2
EPS = 1e-6

S5_CHUNK = 16
NA_QROWS = 4
NA_KROWS = 12
MOE_TM = 512
ROUTER_LANES = 128
VMEM_LIMIT = 56 * 1024 * 1024


def _cparams(sem):
    return pltpu.CompilerParams(dimension_semantics=sem, vmem_limit_bytes=VMEM_LIMIT)


def _sigmoid(x):
    return 1.0 / (1.0 + jnp.exp(-x))


def _tile(prefs, *sizes):
    for c in prefs:
        if all(s % c == 0 for s in sizes):
            return c
    raise ValueError(f"no tile in {prefs} divides {sizes}")


def _prenorm_kernel(xa_ref, xb_ref, g_ref, o_ref, *, na_blocks):
    i = pl.program_id(0)

    def norm(x):
        ms = jnp.mean(x * x, axis=-1, keepdims=True)
        return (x * lax.rsqrt(ms + EPS) * g_ref[...]).astype(o_ref.dtype)

    @pl.when(i < na_blocks)
    def _():
        o_ref[...] = norm(xa_ref[...])

    @pl.when(i >= na_blocks)
    def _():
        o_ref[...] = norm(xb_ref[...])


def _prenorm(xa, xb, g):
    ta, d = xa.shape
    tb = xb.shape[0]
    tm = _tile((512, 256, 128), ta, tb)
    na, nb = ta // tm, tb // tm
    return pl.pallas_call(
        functools.partial(_prenorm_kernel, na_blocks=na),
        out_shape=jax.ShapeDtypeStruct((ta + tb, d), BF16),
        grid=(na + nb,),
        in_specs=[
            pl.BlockSpec((tm, d), lambda i: (jnp.minimum(i, na - 1), 0)),
            pl.BlockSpec((tm, d), lambda i: (jnp.maximum(i - na, 0), 0)),
            pl.BlockSpec((1, d), lambda i: (0, 0)),
        ],
        out_specs=pl.BlockSpec((tm, d), lambda i: (i, 0)),
        compiler_params=_cparams(("arbitrary",)),
        name="prenorm",
    )(xa, xb, g.reshape(1, d).astype(F32))


def _matmul_kernel(a_ref, b_ref, o_ref):
    o_ref[...] = jnp.dot(a_ref[...], b_ref[...], preferred_element_type=F32).astype(o_ref.dtype)


def _matmul(a, b, out_dtype, name):
    m, k = a.shape
    n = b.shape[1]
    tm = _tile((1024, 512, 256, 128), m)
    tn = _tile((1024, 768, 512, 256, 128), n)
    return pl.pallas_call(
        _matmul_kernel,
        out_shape=jax.ShapeDtypeStruct((m, n), out_dtype),
        grid=(m // tm, n // tn),
        in_specs=[
            pl.BlockSpec((tm, k), lambda i, j: (i, 0)),
            pl.BlockSpec((k, tn), lambda i, j: (0, j)),
        ],
        out_specs=pl.BlockSpec((tm, tn), lambda i, j: (i, j)),
        compiler_params=_cparams(("arbitrary", "arbitrary")),
        name=name,
    )(a, b)


def _na_plan(seqs, unit_rows, kb_rows, win_r, rq, kw):
    half = win_r // 2
    keys, dr_l, valid_l, meta, ids = {}, [], [], [], []
    row = 0
    for nseq, srows in seqs:
        assert srows % unit_rows == 0 and srows >= kw >= rq + win_r - 1 and unit_rows % rq == 0
        for _ in range(nseq):
            lo = row % kb_rows
            hi = lo + srows
            assert hi <= kb_rows
            for uu in range(srows // unit_rows):
                qoff = lo + uu * unit_rows
                meta.append((qoff, lo, hi))
                unit_ids = []
                for ib in range(unit_rows // rq):
                    r0 = qoff + ib * rq
                    ks = int(np.clip(r0 - half, lo, hi - kw))
                    r = r0 + np.arange(rq)[:, None]
                    rs = np.clip(r - half, lo, hi - win_r)
                    kr = ks + np.arange(kw)[None, :]
                    valid = (kr >= rs) & (kr < rs + win_r)
                    assert (valid.sum(1) == win_r).all()
                    dr = np.where(valid, kr - r + win_r - 1, 0)
                    key = dr.tobytes() + valid.tobytes()
                    if key not in keys:
                        keys[key] = len(dr_l)
                        dr_l.append(dr)
                        valid_l.append(valid)
                    unit_ids.append(keys[key])
                ids.append(unit_ids)
            row += srows
    return np.stack(dr_l), np.stack(valid_l), np.asarray(meta, np.int32), np.asarray(ids, np.int32)


def _na_bias_tables(rpb, dr, vrow):
    h, nr, nc = rpb.shape
    win_c = (nc + 1) // 2
    w = GRID_W
    p, rq, kw = dr.shape
    qc = np.arange(w)[:, None]
    kc = np.arange(w)[None, :]
    ws = np.clip(qc - win_c // 2, 0, w - win_c)
    vcol = (kc >= ws) & (kc < ws + win_c)
    dc = np.where(vcol, kc - qc + win_c - 1, 0)
    oh_dc = (dc[None] == np.arange(nc)[:, None, None]) & vcol[None]
    oh_dr = (dr[..., None] == np.arange(nr)) & vrow[..., None]
    mask = np.where(vrow[:, :, None, :, None] & vcol[None, None, :, None, :], 0.0, -np.inf)
    mask = mask.reshape(p, 1, rq * w, kw * w).astype(np.float32)
    hi = lax.Precision.HIGHEST
    t1 = jnp.einsum("hrd,dqk->hrqk", rpb.astype(F32), jnp.asarray(oh_dc, F32), precision=hi)
    t2 = jnp.einsum("pair,hrqk->phaqik", jnp.asarray(oh_dr, F32), t1, precision=hi)
    return t2.reshape(p, h, rq * w, kw * w) + jnp.asarray(mask)


def _na_kernel(meta_ref, pat_ref, q_ref, k_ref, v_ref, bias_ref, o_ref, *, rq, kw, half, scale):
    w = GRID_W
    u = pl.program_id(1)
    qoff, lo, hi = meta_ref[u, 0], meta_ref[u, 1], meta_ref[u, 2]
    nblk = q_ref.shape[0] // (rq * w)

    def body(ib, carry):
        ks = jnp.clip(qoff + ib * rq - half, lo, hi - kw)
        q0 = pl.multiple_of(ib * (rq * w), rq * w)
        k0 = pl.multiple_of(ks * w, w)
        q = q_ref[pl.ds(q0, rq * w), :]
        k = k_ref[pl.ds(k0, kw * w), :]
        v = v_ref[pl.ds(k0, kw * w), :]
        s = lax.dot_general(q, k, (((1,), (1,)), ((), ())), preferred_element_type=F32)
        s = s * scale + bias_ref[pat_ref[u, ib], 0]
        m = jnp.max(s, axis=-1, keepdims=True)
        p = jnp.exp(s - m)
        l = jnp.sum(p, axis=-1, keepdims=True)
        o = jnp.dot(p.astype(v.dtype), v, preferred_element_type=F32)
        o_ref[pl.ds(q0, rq * w), :] = (o / l).astype(o_ref.dtype)
        return carry

    lax.fori_loop(0, nblk, body, 0)


def _neighbourhood_attention(z, rpb, seqs, heads):
    t = z.shape[0]
    dh = NA_HEAD_DIM
    win_r = (rpb.shape[1] + 1) // 2
    rq, kw = NA_QROWS, NA_KROWS
    unit_rows = min(r for _, r in seqs)
    kb_rows = max(r for _, r in seqs)
    qb, kb = unit_rows * GRID_W, kb_rows * GRID_W
    assert t % kb == 0 and kb % qb == 0
    dr, vrow, meta, pat = _na_plan(seqs, unit_rows, kb_rows, win_r, rq, kw)
    bias = _na_bias_tables(rpb, dr, vrow)
    p = bias.shape[0]
    n_units = meta.shape[0]
    assert n_units * qb == t
    grid_spec = pltpu.PrefetchScalarGridSpec(
        num_scalar_prefetch=2,
        grid=(heads, n_units),
        in_specs=[
            pl.BlockSpec((qb, dh), lambda h, u, meta, pat: (u, h)),
            pl.BlockSpec((kb, dh), lambda h, u, meta, pat: (u * qb // kb, heads + h)),
            pl.BlockSpec((kb, dh), lambda h, u, meta, pat: (u * qb // kb, 2 * heads + h)),
            pl.BlockSpec((p, 1, rq * GRID_W, kw * GRID_W), lambda h, u, meta, pat: (0, h, 0, 0)),
        ],
        out_specs=pl.BlockSpec((qb, dh), lambda h, u, meta, pat: (u, h)),
    )
    return pl.pallas_call(
        functools.partial(_na_kernel, rq=rq, kw=kw, half=win_r // 2, scale=float(dh) ** -0.5),
        out_shape=jax.ShapeDtypeStruct((t, heads * dh), BF16),
        grid_spec=grid_spec,
        compiler_params=_cparams(("arbitrary", "arbitrary")),
        name="natten",
    )(jnp.asarray(meta), jnp.asarray(pat), z, z, z, bias)


def _s5_matrices(lam_re, lam_im, log_step, b_re, b_im, c_re, c_im, d_skip, tc):
    f32 = F32
    lam_re, lam_im, log_step = lam_re.astype(f32), lam_im.astype(f32), log_step.astype(f32)
    b_re, b_im, c_re, c_im = b_re.astype(f32), b_im.astype(f32), c_re.astype(f32), c_im.astype(f32)
    _, g, p, hg = b_re.shape
    step = jnp.exp(log_step)[..., None]
    er, ei = lam_re * step, lam_im * step
    kk = jnp.arange(tc + 1, dtype=f32)[:, None, None, None]
    mag = jnp.exp(kk * er[None])
    pr, pi = mag * jnp.cos(kk * ei[None]), mag * jnp.sin(kk * ei[None])
    lr, li = pr[1], pi[1]
    den = lam_re * lam_re + lam_im * lam_im
    fr = ((lr - 1.0) * lam_re + li * lam_im) / den
    fi = (li * lam_re - (lr - 1.0) * lam_im) / den
    bbr = fr[..., None] * b_re - fi[..., None] * b_im
    bbi = fr[..., None] * b_im + fi[..., None] * b_re
    clr = c_re[None] * pr[:, :, :, None, :] - c_im[None] * pi[:, :, :, None, :]
    cli = c_re[None] * pi[:, :, :, None, :] + c_im[None] * pr[:, :, :, None, :]
    hi = lax.Precision.HIGHEST
    kern = (jnp.einsum("kdghp,dgpj->kdghj", clr, bbr, precision=hi)
            - jnp.einsum("kdghp,dgpj->kdghj", cli, bbi, precision=hi))

    s_i = np.arange(tc)[:, None]
    t_i = np.arange(tc)[None, :]
    lag_f, lag_b = np.clip(t_i - s_i, 0, tc), np.clip(s_i - t_i, 0, tc)
    mf = kern[lag_f, 0] * jnp.asarray(t_i >= s_i, f32)[:, :, None, None, None]
    mb = kern[lag_b, 1] * jnp.asarray(s_i >= t_i, f32)[:, :, None, None, None]
    m = (mf + mb).transpose(2, 0, 4, 1, 3).reshape(g, tc * hg, tc * hg)

    pf_r, pf_i = pr[tc - 1 - np.arange(tc), 0], pi[tc - 1 - np.arange(tc), 0]
    pb_r, pb_i = pr[np.arange(tc), 1], pi[np.arange(tc), 1]

    def inc(qr, qi, d):
        re = qr[..., None] * bbr[d][None] - qi[..., None] * bbi[d][None]
        im = qr[..., None] * bbi[d][None] + qi[..., None] * bbr[d][None]
        return re, im

    f_re, f_im = inc(pf_r, pf_i, 0)
    b_re_, b_im_ = inc(pb_r, pb_i, 1)
    ws_g = jnp.stack([f_re, f_im, b_re_, b_im_], 0).transpose(2, 1, 4, 0, 3)
    ws_g = ws_g.reshape(g // 2, 2, tc * hg, 4, p)
    eye2 = jnp.eye(2, dtype=f32)
    ws = (ws_g[:, :, :, :, None, :] * eye2[None, :, None, None, :, None]).reshape(g // 2, 2 * tc * hg, 8 * p)

    ef, eb = np.arange(tc) + 1, tc - np.arange(tc)
    wo_g = jnp.stack([clr[ef, 0], -cli[ef, 0], clr[eb, 1], -cli[eb, 1]], 0)
    wo_g = wo_g.transpose(2, 0, 4, 1, 3).reshape(g // 2, 2, 4, p, tc * hg)
    wo = (wo_g.transpose(0, 2, 1, 3, 4)[:, :, :, :, None, :]
          * eye2[None, None, :, None, :, None]).reshape(g // 2, 8 * p, 2 * tc * hg)

    a = jnp.stack([pr[tc, 0], pi[tc, 0], pr[tc, 1], pi[tc, 1]], 0).reshape(4, g * p)
    dsk = jnp.broadcast_to(d_skip.astype(f32).reshape(g, 1, hg), (g, tc, hg)).reshape(1, g * tc * hg)
    return m.astype(BF16), ws.astype(BF16), wo.astype(BF16), a, dsk


def _s5_kernel(u_ref, m_ref, ws_ref, wo_ref, a_ref, dsk_ref, o_ref, s_ref, x_ref, *, segments, rc):
    nrows, cw = u_ref.shape
    hw = cw // 2
    lw = s_ref.shape[1] // 4

    def rows_loop(fn):
        def body(i, c):
            fn(pl.multiple_of(i * rc, rc))
            return c
        lax.fori_loop(0, nrows // rc, body, 0)

    def state_inc(r0):
        s_ref[pl.ds(r0, rc), :] = jnp.dot(u_ref[pl.ds(r0, rc), :], ws_ref[0], preferred_element_type=F32)

    rows_loop(state_inc)

    afr, afi, abr, abi = a_ref[0:1, :], a_ref[1:2, :], a_ref[2:3, :], a_ref[3:4, :]
    sub = 8
    for first, nsteps, rps in segments:
        per_tile = 1 if rps % sub == 0 else sub
        assert rps % sub == 0 or (rps == 1 and nsteps % sub == 0 and first % sub == 0)
        tr = rps * per_tile
        ntiles = nsteps // per_tile

        def tile_step(c, carry, first=first, ntiles=ntiles, per_tile=per_tile, tr=tr):
            fr, fi, br, bi = carry
            rf = pl.multiple_of(first + c * tr, sub)
            rb = pl.multiple_of(first + (ntiles - 1 - c) * tr, sub)
            sfr = s_ref[pl.ds(rf, tr), 0:lw]
            sfi = s_ref[pl.ds(rf, tr), lw:2 * lw]
            sbr = s_ref[pl.ds(rb, tr), 2 * lw:3 * lw]
            sbi = s_ref[pl.ds(rb, tr), 3 * lw:4 * lw]
            if per_tile == 1:
                xfr, xfi, xbr, xbi = fr, fi, br, bi
                fr, fi = afr * fr - afi * fi + sfr, afr * fi + afi * fr + sfi
                br, bi = abr * br - abi * bi + sbr, abr * bi + abi * br + sbi
            else:
                row = lax.broadcasted_iota(jnp.int32, (tr, lw), 0)
                xfr = xfi = xbr = xbi = jnp.zeros((tr, lw), F32)
                for j in range(per_tile):
                    jb = per_tile - 1 - j
                    xfr, xfi = jnp.where(row == j, fr, xfr), jnp.where(row == j, fi, xfi)
                    xbr, xbi = jnp.where(row == jb, br, xbr), jnp.where(row == jb, bi, xbi)
                    fr, fi = (afr * fr - afi * fi + sfr[j:j + 1], afr * fi + afi * fr + sfi[j:j + 1])
                    br, bi = (abr * br - abi * bi + sbr[jb:jb + 1], abr * bi + abi * br + sbi[jb:jb + 1])
            x_ref[pl.ds(rf, tr), 0:lw] = xfr
            x_ref[pl.ds(rf, tr), lw:2 * lw] = xfi
            x_ref[pl.ds(rb, tr), 2 * lw:3 * lw] = xbr
            x_ref[pl.ds(rb, tr), 3 * lw:4 * lw] = xbi
            return fr, fi, br, bi

        zero = jnp.zeros((rps, lw), F32)
        lax.fori_loop(0, ntiles, tile_step, (zero, zero, zero, zero))

    def outputs(r0):
        u = u_ref[pl.ds(r0, rc), :]
        y = jnp.dot(x_ref[pl.ds(r0, rc), :].astype(BF16), wo_ref[0], preferred_element_type=F32)
        y = y + dsk_ref[...] * u.astype(F32)
        ya = y[:, :hw] + jnp.dot(u[:, :hw], m_ref[0], preferred_element_type=F32)
        yb = y[:, hw:] + jnp.dot(u[:, hw:], m_ref[1], preferred_element_type=F32)
        o_ref[pl.ds(r0, rc), 0:hw] = jax.nn.gelu(ya, approximate=True).astype(o_ref.dtype)
        o_ref[pl.ds(r0, rc), hw:cw] = jax.nn.gelu(yb, approximate=True).astype(o_ref.dtype)

    rows_loop(outputs)


def _s5_mixer(u_t, mats, segments):
    m, ws, wo, a, dsk = mats
    nrows, width = u_t.shape
    npairs = ws.shape[0]
    cw = width // npairs
    sw = ws.shape[2]
    rc = _tile((512, 256, 128, 64), nrows)
    return pl.pallas_call(
        functools.partial(_s5_kernel, segments=segments, rc=rc),
        out_shape=jax.ShapeDtypeStruct((nrows, width), BF16),
        grid=(npairs,),
        in_specs=[
            pl.BlockSpec((nrows, cw), lambda g: (0, g)),
            pl.BlockSpec((2, cw // 2, cw // 2), lambda g: (g, 0, 0)),
            pl.BlockSpec((1, cw, sw), lambda g: (g, 0, 0)),
            pl.BlockSpec((1, sw, cw), lambda g: (g, 0, 0)),
            pl.BlockSpec((4, sw // 4), lambda g: (0, g)),
            pl.BlockSpec((1, cw), lambda g: (0, g)),
        ],
        out_specs=pl.BlockSpec((nrows, cw), lambda g: (0, g)),
        scratch_shapes=[pltpu.VMEM((nrows, sw), F32), pltpu.VMEM((nrows, sw), F32)],
        compiler_params=_cparams(("arbitrary",)),
        name="s5_mixer",
    )(u_t, m, ws, wo, a, dsk)


def _mix_kernel(att_ref, y5_ref, ga_ref, gb_ref, wglu_ref, wpa_ref, wpb_ref, o_ref, glu_ref):
    @pl.when(pl.program_id(1) == 0)
    def _():
        gl = jnp.dot(y5_ref[...], wglu_ref[...], preferred_element_type=F32)
        sw = gl.shape[1] // 2
        glu_ref[...] = (gl[:, :sw] * _sigmoid(gl[:, sw:])).astype(glu_ref.dtype)

    ya = jnp.dot(att_ref[...], wpa_ref[...], preferred_element_type=F32)
    yb = jnp.dot(glu_ref[...], wpb_ref[...], preferred_element_type=F32)
    o = _sigmoid(ga_ref[...].astype(F32)) * ya + _sigmoid(gb_ref[...].astype(F32)) * yb
    o_ref[...] = o.astype(o_ref.dtype)


def _mix(att, y5, z, ga_col, gb_col, w_glu, w_pa, w_pb):
    t, na = att.shape
    sw = y5.shape[1]
    d = w_pa.shape[1]
    tm = _tile((512, 256, 128), t)
    tn = _tile((1024, 512, 256, 128), d, ga_col, gb_col)
    return pl.pallas_call(
        _mix_kernel,
        out_shape=jax.ShapeDtypeStruct((t, d), BF16),
        grid=(t // tm, d // tn),
        in_specs=[
            pl.BlockSpec((tm, na), lambda i, j: (i, 0)),
            pl.BlockSpec((tm, sw), lambda i, j: (i, 0)),
            pl.BlockSpec((tm, tn), lambda i, j: (i, ga_col // tn + j)),
            pl.BlockSpec((tm, tn), lambda i, j: (i, gb_col // tn + j)),
            pl.BlockSpec((sw, 2 * sw), lambda i, j: (0, 0)),
            pl.BlockSpec((na, tn), lambda i, j: (0, j)),
            pl.BlockSpec((sw, tn), lambda i, j: (0, j)),
        ],
        out_specs=pl.BlockSpec((tm, tn), lambda i, j: (i, j)),
        scratch_shapes=[pltpu.VMEM((tm, sw), BF16)],
        compiler_params=_cparams(("arbitrary", "arbitrary")),
        name="mixer_merge",
    )(att, y5, z, z, w_glu, w_pa, w_pb)


def _outproj_kernel(mix_ref, xa_ref, xb_ref, wo_ref, g2_ref, wr_ref, x2_ref, h2_ref, lg_ref, *, na_blocks):
    i = pl.program_id(0)
    acc = jnp.dot(mix_ref[...], wo_ref[...], preferred_element_type=F32)

    def finish(x):
        x2 = x + acc
        x2_ref[...] = x2
        ms = jnp.mean(x2 * x2, axis=-1, keepdims=True)
        h2 = x2 * lax.rsqrt(ms + EPS) * g2_ref[...]
        hb = h2.astype(BF16)
        h2_ref[...] = hb
        hl = (h2 - hb.astype(F32)).astype(BF16)
        lg = jnp.dot(hb, wr_ref[0], preferred_element_type=F32)
        lg = lg + jnp.dot(hl, wr_ref[0], preferred_element_type=F32)
        lg = lg + jnp.dot(hb, wr_ref[1], preferred_element_type=F32)
        lg_ref[...] = lg

    @pl.when(i < na_blocks)
    def _():
        finish(xa_ref[...])

    @pl.when(i >= na_blocks)
    def _():
        finish(xb_ref[...])


def _outproj(mixed, xa, xb, w_o, g2, w_router):
    t, d = mixed.shape
    tm = _tile((256, 128), xa.shape[0], xb.shape[0])
    na = xa.shape[0] // tm
    nr = w_router.shape[2]
    return pl.pallas_call(
        functools.partial(_outproj_kernel, na_blocks=na),
        out_shape=(jax.ShapeDtypeStruct((t, d), F32), jax.ShapeDtypeStruct((t, d), BF16),
                   jax.ShapeDtypeStruct((t, nr), F32)),
        grid=(t // tm,),
        in_specs=[
            pl.BlockSpec((tm, d), lambda i: (i, 0)),
            pl.BlockSpec((tm, d), lambda i: (jnp.minimum(i, na - 1), 0)),
            pl.BlockSpec((tm, d), lambda i: (jnp.maximum(i - na, 0), 0)),
            pl.BlockSpec((d, d), lambda i: (0, 0)),
            pl.BlockSpec((1, d), lambda i: (0, 0)),
            pl.BlockSpec((2, d, nr), lambda i: (0, 0, 0)),
        ],
        out_specs=(pl.BlockSpec((tm, d), lambda i: (i, 0)), pl.BlockSpec((tm, d), lambda i: (i, 0)),
                   pl.BlockSpec((tm, nr), lambda i: (i, 0))),
        compiler_params=_cparams(("arbitrary",)),
        name="outproj_norm_router",
    )(mixed, xa, xb, w_o, g2.reshape(1, d).astype(F32), w_router)


def _route(logits, b_rg, b_re, n_groups, n_experts, tm):
    t = logits.shape[0]
    epg = n_experts // n_groups
    gl = logits[:, :n_groups] + b_rg.astype(F32)
    g_prob = jax.nn.softmax(gl, axis=-1)
    g_idx = jnp.argmax(g_prob, axis=-1)
    g_w = jnp.max(g_prob, axis=-1)
    el = (logits[:, n_groups:n_groups + n_experts] + b_re.astype(F32)).reshape(t, n_groups, epg)
    g_oh = g_idx[:, None] == jnp.arange(n_groups)[None, :]
    e_sel = jnp.sum(jnp.where(g_oh[:, :, None], el, 0.0), axis=1)
    e_w, e_idx = lax.top_k(jax.nn.softmax(e_sel, axis=-1), TOP_K)
    gates = g_w[:, None] * (e_w / jnp.sum(e_w, axis=-1, keepdims=True))
    expert = (g_idx[:, None] * epg + e_idx).astype(jnp.int32)

    m = t * TOP_K
    eid = expert.reshape(m)
    oh = (eid[:, None] == jnp.arange(n_experts, dtype=jnp.int32)[None, :]).astype(jnp.int32)
    cum = jnp.cumsum(oh, axis=0)
    rank = jnp.sum(cum * oh, axis=1) - 1
    counts = cum[-1]
    padded = (counts + tm - 1) // tm * tm
    p_ends = jnp.cumsum(padded)
    p_starts = p_ends - padded
    dest = jnp.sum(oh * p_starts[None, :], axis=1) + rank
    n_blocks = m // tm + n_experts
    blk_start = jnp.arange(n_blocks, dtype=jnp.int32) * tm
    blk_eid = jnp.minimum(jnp.sum((p_ends[None, :] <= blk_start[:, None]).astype(jnp.int32), axis=1),
                          n_experts - 1).astype(jnp.int32)
    blk_valid = (blk_start < p_ends[-1]).astype(jnp.int32)
    tok = jnp.arange(m, dtype=jnp.int32) // TOP_K
    buf_tok = jnp.zeros((n_blocks * tm,), jnp.int32).at[dest].set(tok)
    return gates, dest.reshape(t, TOP_K), buf_tok, blk_eid, blk_valid


def _moe_kernel(eid_ref, valid_ref, x_ref, wg_ref, wu_ref, wd_ref, o_ref):
    i = pl.program_id(0)

    @pl.when(valid_ref[i] != 0)
    def _():
        x = x_ref[...]
        g = jnp.dot(x, wg_ref[...], preferred_element_type=F32)
        u = jnp.dot(x, wu_ref[...], preferred_element_type=F32)
        hid = (g * _sigmoid(g) * u).astype(BF16)
        o_ref[...] = jnp.dot(hid, wd_ref[...], preferred_element_type=F32).astype(o_ref.dtype)

    @pl.when(valid_ref[i] == 0)
    def _():
        o_ref[...] = jnp.zeros(o_ref.shape, o_ref.dtype)


def _moe_experts(x_buf, blk_eid, blk_valid, w_gate, w_up, w_down, tm):
    n, d = x_buf.shape
    ff = w_gate.shape[2]
    grid_spec = pltpu.PrefetchScalarGridSpec(
        num_scalar_prefetch=2,
        grid=(n // tm,),
        in_specs=[
            pl.BlockSpec((tm, d), lambda i, eid, valid: (i, 0)),
            pl.BlockSpec((None, d, ff), lambda i, eid, valid: (eid[i], 0, 0)),
            pl.BlockSpec((None, d, ff), lambda i, eid, valid: (eid[i], 0, 0)),
            pl.BlockSpec((None, ff, d), lambda i, eid, valid: (eid[i], 0, 0)),
        ],
        out_specs=pl.BlockSpec((tm, d), lambda i, eid, valid: (i, 0)),
    )
    return pl.pallas_call(
        _moe_kernel,
        out_shape=jax.ShapeDtypeStruct((n, d), BF16),
        grid_spec=grid_spec,
        compiler_params=_cparams(("arbitrary",)),
        name="moe_experts",
    )(blk_eid, blk_valid, x_buf, w_gate, w_up, w_down)


def _combine_kernel(x_ref, y0_ref, y1_ref, gt_ref, fg_ref, o_ref, *, final_norm):
    g = gt_ref[...]
    x = x_ref[...] + g[:, 0:1] * y0_ref[...].astype(F32) + g[:, 1:2] * y1_ref[...].astype(F32)
    if final_norm:
        ms = jnp.mean(x * x, axis=-1, keepdims=True)
        x = x * lax.rsqrt(ms + EPS) * fg_ref[...]
    o_ref[...] = x


def _combine(x2, y0, y1, gates, final_g, row0, nrows, final_norm):
    d = x2.shape[1]
    tm = _tile((512, 256, 128), row0, nrows) if row0 else _tile((512, 256, 128), nrows)
    b0 = row0 // tm
    return pl.pallas_call(
        functools.partial(_combine_kernel, final_norm=final_norm),
        out_shape=jax.ShapeDtypeStruct((nrows, d), F32),
        grid=(nrows // tm,),
        in_specs=[
            pl.BlockSpec((tm, d), lambda i: (b0 + i, 0)),
            pl.BlockSpec((tm, d), lambda i: (b0 + i, 0)),
            pl.BlockSpec((tm, d), lambda i: (b0 + i, 0)),
            pl.BlockSpec((tm, TOP_K), lambda i: (b0 + i, 0)),
            pl.BlockSpec((1, d), lambda i: (0, 0)),
        ],
        out_specs=pl.BlockSpec((tm, d), lambda i: (i, 0)),
        compiler_params=_cparams(("arbitrary",)),
        name="moe_combine",
    )(x2, y0, y1, gates, final_g.reshape(1, d).astype(F32))


def _layer(xa, xb, seq_a, seq_b, p, final_g, is_last):
    ta, d = xa.shape
    tb = xb.shape[0]
    heads = p["rpb"].shape[0]
    na_w = heads * NA_HEAD_DIM
    _, g5, _, hg = p["b_re"].shape
    s5_w = g5 * hg
    tc = S5_CHUNK
    assert ta == seq_a, "the first stream is a single sequence"
    nb = tb // seq_b

    h = _prenorm(xa, xb, p["norm1_g"])
    z = _matmul(h, p["w_in"].astype(BF16), BF16, "in_proj")

    att = _neighbourhood_attention(z, p["rpb"], ((1, seq_a // GRID_W), (nb, seq_b // GRID_W)), heads)

    u = z[:, 3 * na_w:3 * na_w + s5_w]
    ca, cb = ta // tc, seq_b // tc
    ua = u[:ta].reshape(ca, tc, g5, hg).transpose(0, 2, 1, 3).reshape(ca, g5 * tc * hg)
    ub = u[ta:].reshape(nb, cb, tc, g5, hg).transpose(1, 0, 3, 2, 4).reshape(cb * nb, g5 * tc * hg)
    mats = _s5_matrices(p["lam_re"], p["lam_im"], p["log_step"], p["b_re"], p["b_im"], p["c_re"], p["c_im"],
                        p["d_skip"], tc)
    y5_t = _s5_mixer(jnp.concatenate([ua, ub], axis=0), mats, ((0, ca, 1), (ca, cb, nb)))
    ya = y5_t[:ca].reshape(ca, g5, tc, hg).transpose(0, 2, 1, 3).reshape(ta, s5_w)
    yb = y5_t[ca:].reshape(cb, nb, g5, tc, hg).transpose(1, 0, 3, 2, 4).reshape(tb, s5_w)
    y5 = jnp.concatenate([ya, yb], axis=0)

    mixed = _mix(att, y5, z, 3 * na_w + s5_w, 3 * na_w + s5_w + d, p["w_glu"].astype(BF16),
                 p["w_branch_a"].astype(BF16), p["w_branch_b"].astype(BF16))

    n_groups, n_experts = p["w_route_g"].shape[1], p["w_route_e"].shape[1]
    wr = jnp.concatenate([p["w_route_g"].astype(F32), p["w_route_e"].astype(F32),
                          jnp.zeros((d, ROUTER_LANES - n_groups - n_experts), F32)], axis=1)
    wr_hi = wr.astype(BF16)
    wr_lo = (wr - wr_hi.astype(F32)).astype(BF16)
    x2, h2, logits = _outproj(mixed, xa, xb, p["w_out"].astype(BF16), p["norm2_g"], jnp.stack([wr_hi, wr_lo]))

    gates, dest, buf_tok, blk_eid, blk_valid = _route(logits, p["b_route_g"], p["b_route_e"], n_groups,
                                                     n_experts, MOE_TM)
    x_buf = jnp.take(h2, buf_tok, axis=0)
    y_buf = _moe_experts(x_buf, blk_eid, blk_valid, p["w_gate"].astype(BF16), p["w_up"].astype(BF16),
                         p["w_down"].astype(BF16), MOE_TM)
    y0 = jnp.take(y_buf, dest[:, 0], axis=0)
    y1 = jnp.take(y_buf, dest[:, 1], axis=0)
    oa = _combine(x2, y0, y1, gates, final_g, 0, ta, is_last)
    ob = _combine(x2, y0, y1, gates, final_g, ta, tb, is_last)
    return oa, ob


_LAYER_PARAMS = ("norm1_g", "w_in", "rpb", "lam_re", "lam_im", "log_step", "b_re", "b_im", "c_re", "c_im",
                 "d_skip", "w_glu", "w_branch_a", "w_branch_b", "w_out", "norm2_g", "w_route_g", "b_route_g",
                 "w_route_e", "b_route_e", "w_gate", "w_up", "w_down")


def kernel(x_prompt, x_sample, norm1_g, w_in, rpb, lam_re, lam_im, log_step, b_re, b_im, c_re, c_im, d_skip,
           w_glu, w_branch_a, w_branch_b, w_out, norm2_g, w_route_g, b_route_g, w_route_e, b_route_e, w_gate,
           w_up, w_down, final_g):
    stacked = dict(zip(_LAYER_PARAMS, (norm1_g, w_in, rpb, lam_re, lam_im, log_step, b_re, b_im, c_re, c_im,
                                       d_skip, w_glu, w_branch_a, w_branch_b, w_out, norm2_g, w_route_g,
                                       b_route_g, w_route_e, b_route_e, w_gate, w_up, w_down)))
    depth = norm1_g.shape[0]
    ba, seq_a, d = x_prompt.shape
    bb, seq_b, _ = x_sample.shape
    xa = x_prompt.reshape(ba * seq_a, d)
    xb = x_sample.reshape(bb * seq_b, d)
    for l in range(depth):
        layer_p = {k: v[l] for k, v in stacked.items()}
        xa, xb = _layer(xa, xb, seq_a, seq_b, layer_p, final_g, l == depth - 1)
    return xa.reshape(ba, seq_a, d), xb.reshape(bb, seq_b, d)
```

```python
import functools

import numpy as np
import jax
import jax.numpy as jnp
from jax import lax
from jax.experimental import pallas as pl
from jax.experimental.pallas import tpu as pltpu

F32 = jnp.float32
BF16 = jnp.bfloat16

GRID_W = 64
NA_HEAD_DIM = 128
TOP_K = 2
EPS = 1e-6

LANES = 128
SUBLANES = 8

S5_CHUNK = 16
NA_QROWS = 4
NA_KROWS = 12
MOE_TM = 512
ROUTER_LANES = LANES
VMEM_LIMIT = 56 * 1024 * 1024


def _cparams(sem):
    return pltpu.CompilerParams(dimension_semantics=sem, vmem_limit_bytes=VMEM_LIMIT)


def _sigmoid(x):
    return 1.0 / (1.0 + jnp.exp(-x))


def _tile(prefs, *sizes):
    for c in prefs:
        if all(s % c == 0 for s in sizes):
            return c
    raise ValueError(f"no tile in {prefs} divides {sizes}")


def _cast_kernel(x_ref, o_ref):
    o_ref[...] = x_ref[...].astype(o_ref.dtype)


def _to_bf16(w):
    shape = w.shape
    cols = shape[-1]
    rows = w.size // cols
    tr = _tile((2048, 1024, 512, 256, 128, 64, 32, 16), rows)
    while tr * cols * 4 > 4 * 1024 * 1024 and tr % 32 == 0:
        tr //= 2
    out = pl.pallas_call(
        _cast_kernel,
        out_shape=jax.ShapeDtypeStruct((rows, cols), BF16),
        grid=(rows // tr,),
        in_specs=[pl.BlockSpec((tr, cols), lambda i: (i, 0))],
        out_specs=pl.BlockSpec((tr, cols), lambda i: (i, 0)),
        compiler_params=_cparams(("arbitrary",)),
        name="cast_bf16",
    )(w.reshape(rows, cols))
    return out.reshape(shape)


def _prenorm_kernel(xa_ref, xb_ref, g_ref, o_ref, *, na_blocks):
    i = pl.program_id(0)

    def norm(x):
        ms = jnp.mean(x * x, axis=-1, keepdims=True)
        return (x * lax.rsqrt(ms + EPS) * g_ref[...]).astype(o_ref.dtype)

    @pl.when(i < na_blocks)
    def _():
        o_ref[...] = norm(xa_ref[...])

    @pl.when(i >= na_blocks)
    def _():
        o_ref[...] = norm(xb_ref[...])


def _prenorm(xa, xb, g):
    ta, d = xa.shape
    tb = xb.shape[0]
    tm = _tile((512, 256, 128), ta, tb)
    na, nb = ta // tm, tb // tm
    return pl.pallas_call(
        functools.partial(_prenorm_kernel, na_blocks=na),
        out_shape=jax.ShapeDtypeStruct((ta + tb, d), BF16),
        grid=(na + nb,),
        in_specs=[
            pl.BlockSpec((tm, d), lambda i: (jnp.minimum(i, na - 1), 0)),
            pl.BlockSpec((tm, d), lambda i: (jnp.maximum(i - na, 0), 0)),
            pl.BlockSpec((1, d), lambda i: (0, 0)),
        ],
        out_specs=pl.BlockSpec((tm, d), lambda i: (i, 0)),
        compiler_params=_cparams(("arbitrary",)),
        name="prenorm",
    )(xa, xb, g.reshape(1, d).astype(F32))


def _matmul_kernel(a_ref, b_ref, o_ref):
    o_ref[...] = jnp.dot(a_ref[...], b_ref[...], preferred_element_type=F32).astype(o_ref.dtype)


def _matmul(a, b, out_dtype, name):
    m, k = a.shape
    n = b.shape[1]
    tm = _tile((1024, 512, 256, 128), m)
    tn = _tile((1024, 768, 512, 256, 128), n)
    return pl.pallas_call(
        _matmul_kernel,
        out_shape=jax.ShapeDtypeStruct((m, n), out_dtype),
        grid=(m // tm, n // tn),
        in_specs=[
            pl.BlockSpec((tm, k), lambda i, j: (i, 0)),
            pl.BlockSpec((k, tn), lambda i, j: (0, j)),
        ],
        out_specs=pl.BlockSpec((tm, tn), lambda i, j: (i, j)),
        compiler_params=_cparams(("arbitrary", "arbitrary")),
        name=name,
    )(a, b)


def _na_plan(seqs, unit_rows, kb_rows, win_r, rq, kw):
    half = win_r // 2
    keys, dr_l, valid_l, meta, ids = {}, [], [], [], []
    row = 0
    for nseq, srows in seqs:
        assert srows % unit_rows == 0 and srows >= kw >= rq + win_r - 1 and unit_rows % rq == 0
        for _ in range(nseq):
            lo = row % kb_rows
            hi = lo + srows
            assert hi <= kb_rows
            for uu in range(srows // unit_rows):
                qoff = lo + uu * unit_rows
                meta.append((qoff, lo, hi))
                unit_ids = []
                for ib in range(unit_rows // rq):
                    r0 = qoff + ib * rq
                    ks = int(np.clip(r0 - half, lo, hi - kw))
                    r = r0 + np.arange(rq)[:, None]
                    rs = np.clip(r - half, lo, hi - win_r)
                    kr = ks + np.arange(kw)[None, :]
                    valid = (kr >= rs) & (kr < rs + win_r)
                    assert (valid.sum(1) == win_r).all()
                    dr = np.where(valid, kr - r + win_r - 1, 0)
                    key = dr.tobytes() + valid.tobytes()
                    if key not in keys:
                        keys[key] = len(dr_l)
                        dr_l.append(dr)
                        valid_l.append(valid)
                    unit_ids.append(keys[key])
                ids.append(unit_ids)
            row += srows
    return np.stack(dr_l), np.stack(valid_l), np.asarray(meta, np.int32), np.asarray(ids, np.int32)


def _na_bias_tables(rpb, dr, vrow):
    h, nr, nc = rpb.shape
    win_c = (nc + 1) // 2
    w = GRID_W
    p, rq, kw = dr.shape
    qc = np.arange(w)[:, None]
    kc = np.arange(w)[None, :]
    ws = np.clip(qc - win_c // 2, 0, w - win_c)
    vcol = (kc >= ws) & (kc < ws + win_c)
    dc = np.where(vcol, kc - qc + win_c - 1, 0)
    oh_dc = (dc[None] == np.arange(nc)[:, None, None]) & vcol[None]
    oh_dr = (dr[..., None] == np.arange(nr)) & vrow[..., None]
    mask = np.where(vrow[:, :, None, :, None] & vcol[None, None, :, None, :], 0.0, -np.inf)
    mask = mask.reshape(p, 1, rq * w, kw * w).astype(np.float32)
    hi = lax.Precision.HIGHEST
    t1 = jnp.einsum("hrd,dqk->hrqk", rpb.astype(F32), jnp.asarray(oh_dc, F32), precision=hi)
    t2 = jnp.einsum("pair,hrqk->phaqik", jnp.asarray(oh_dr, F32), t1, precision=hi)
    return t2.reshape(p, h, rq * w, kw * w) + jnp.asarray(mask)


def _na_kernel(meta_ref, pat_ref, q_ref, k_ref, v_ref, bias_ref, o_ref, *, rq, kw, half, scale):
    w = GRID_W
    u = pl.program_id(1)
    qoff, lo, hi = meta_ref[u, 0], meta_ref[u, 1], meta_ref[u, 2]
    nblk = q_ref.shape[0] // (rq * w)

    def body(ib, carry):
        ks = jnp.clip(qoff + ib * rq - half, lo, hi - kw)
        q0 = pl.multiple_of(ib * (rq * w), rq * w)
        k0 = pl.multiple_of(ks * w, w)
        q = q_ref[pl.ds(q0, rq * w), :]
        k = k_ref[pl.ds(k0, kw * w), :]
        v = v_ref[pl.ds(k0, kw * w), :]
        s = lax.dot_general(q, k, (((1,), (1,)), ((), ())), preferred_element_type=F32)
        s = s * scale + bias_ref[pat_ref[u, ib], 0]
        m = jnp.max(s, axis=-1, keepdims=True)
        p = jnp.exp(s - m)
        l = jnp.sum(p, axis=-1, keepdims=True)
        o = jnp.dot(p.astype(v.dtype), v, preferred_element_type=F32)
        o_ref[pl.ds(q0, rq * w), :] = (o / l).astype(o_ref.dtype)
        return carry

    lax.fori_loop(0, nblk, body, 0, unroll=2)


def _neighbourhood_attention(z, rpb, seqs, heads):
    t = z.shape[0]
    dh = NA_HEAD_DIM
    win_r = (rpb.shape[1] + 1) // 2
    rq, kw = NA_QROWS, NA_KROWS
    unit_rows = min(r for _, r in seqs)
    kb_rows = max(r for _, r in seqs)
    qb, kb = unit_rows * GRID_W, kb_rows * GRID_W
    assert t % kb == 0 and kb % qb == 0
    dr, vrow, meta, pat = _na_plan(seqs, unit_rows, kb_rows, win_r, rq, kw)
    bias = _na_bias_tables(rpb, dr, vrow)
    p = bias.shape[0]
    n_units = meta.shape[0]
    assert n_units * qb == t
    grid_spec = pltpu.PrefetchScalarGridSpec(
        num_scalar_prefetch=2,
        grid=(heads, n_units),
        in_specs=[
            pl.BlockSpec((qb, dh), lambda h, u, meta, pat: (u, h)),
            pl.BlockSpec((kb, dh), lambda h, u, meta, pat: (u * qb // kb, heads + h)),
            pl.BlockSpec((kb, dh), lambda h, u, meta, pat: (u * qb // kb, 2 * heads + h)),
            pl.BlockSpec((p, 1, rq * GRID_W, kw * GRID_W), lambda h, u, meta, pat: (0, h, 0, 0)),
        ],
        out_specs=pl.BlockSpec((qb, dh), lambda h, u, meta, pat: (u, h)),
    )
    return pl.pallas_call(
        functools.partial(_na_kernel, rq=rq, kw=kw, half=win_r // 2, scale=float(dh) ** -0.5),
        out_shape=jax.ShapeDtypeStruct((t, heads * dh), BF16),
        grid_spec=grid_spec,
        compiler_params=_cparams(("arbitrary", "arbitrary")),
        name="natten",
    )(jnp.asarray(meta), jnp.asarray(pat), z, z, z, bias)


def _s5_matrices(lam_re, lam_im, log_step, b_re, b_im, c_re, c_im, tc):
    f32 = F32
    lam_re, lam_im, log_step = lam_re.astype(f32), lam_im.astype(f32), log_step.astype(f32)
    b_re, b_im, c_re, c_im = b_re.astype(f32), b_im.astype(f32), c_re.astype(f32), c_im.astype(f32)
    _, g, p, hg = b_re.shape
    gt = LANES // hg
    nt = g // gt
    assert LANES % hg == 0 and g % gt == 0
    step = jnp.exp(log_step)[..., None]
    er, ei = lam_re * step, lam_im * step
    kk = jnp.arange(tc + 1, dtype=f32)[:, None, None, None]
    mag = jnp.exp(kk * er[None])
    pr, pi = mag * jnp.cos(kk * ei[None]), mag * jnp.sin(kk * ei[None])
    lr, li = pr[1], pi[1]
    den = lam_re * lam_re + lam_im * lam_im
    fr = ((lr - 1.0) * lam_re + li * lam_im) / den
    fi = (li * lam_re - (lr - 1.0) * lam_im) / den
    bbr = fr[..., None] * b_re - fi[..., None] * b_im
    bbi = fr[..., None] * b_im + fi[..., None] * b_re
    clr = c_re[None] * pr[:, :, :, None, :] - c_im[None] * pi[:, :, :, None, :]
    cli = c_re[None] * pi[:, :, :, None, :] + c_im[None] * pr[:, :, :, None, :]
    hi = lax.Precision.HIGHEST
    kern = (jnp.einsum("kdghp,dgpj->kdghj", clr, bbr, precision=hi)
            - jnp.einsum("kdghp,dgpj->kdghj", cli, bbi, precision=hi))
    eye = jnp.eye(gt, dtype=f32)

    s_i = np.arange(tc)[:, None]
    t_i = np.arange(tc)[None, :]
    lag_f, lag_b = np.clip(t_i - s_i, 0, tc), np.clip(s_i - t_i, 0, tc)
    mf = kern[lag_f, 0] * jnp.asarray(t_i >= s_i, f32)[:, :, None, None, None]
    mb = kern[lag_b, 1] * jnp.asarray(s_i >= t_i, f32)[:, :, None, None, None]
    m_g = (mf + mb).reshape(tc, tc, nt, gt, hg, hg)
    m = jnp.einsum("stagkh,gj->asghtjk", m_g, eye).reshape(nt, tc * LANES, tc * LANES)

    pf_r, pf_i = pr[tc - 1 - np.arange(tc), 0], pi[tc - 1 - np.arange(tc), 0]
    pb_r, pb_i = pr[np.arange(tc), 1], pi[np.arange(tc), 1]

    def inc(qr, qi, d):
        re = qr[..., None] * bbr[d][None] - qi[..., None] * bbi[d][None]
        im = qr[..., None] * bbi[d][None] + qi[..., None] * bbr[d][None]
        return re, im

    f_re, f_im = inc(pf_r, pf_i, 0)
    b_re_, b_im_ = inc(pb_r, pb_i, 1)
    ws_g = jnp.stack([f_re, f_im, b_re_, b_im_], 0).reshape(4, tc, nt, gt, p, hg)
    ws = jnp.einsum("csagph,gj->asghcjp", ws_g, eye).reshape(nt, tc * LANES, 4 * gt * p)

    ef, eb = np.arange(tc) + 1, tc - np.arange(tc)
    wo_g = jnp.stack([clr[ef, 0], -cli[ef, 0], clr[eb, 1], -cli[eb, 1]], 0)
    wo_g = wo_g.reshape(4, tc, nt, gt, hg, p)
    wo = jnp.einsum("ctagkp,gj->acjptgk", wo_g, eye).reshape(nt, 4 * gt * p, tc * LANES)

    a = jnp.stack([pr[tc, 0], pi[tc, 0], pr[tc, 1], pi[tc, 1]], 0).reshape(4, g * p)
    return m.astype(BF16), ws.astype(BF16), wo.astype(BF16), a


def _s5_kernel(start_ref, end_ref, u_ref, m_ref, ws_ref, wo_ref, a_ref, dsk_ref, o_ref, s_ref, x_ref):
    tc, rb, _ = u_ref.shape
    lw = a_ref.shape[1]
    ntiles = rb // SUBLANES
    tile0 = pl.program_id(1) * ntiles

    u = jnp.concatenate([u_ref[s] for s in range(tc)], axis=1)
    s_ref[...] = jnp.dot(u, ws_ref[...], preferred_element_type=F32)

    afr, afi, abr, abi = a_ref[0:1, :], a_ref[1:2, :], a_ref[2:3, :], a_ref[3:4, :]
    row = lax.broadcasted_iota(jnp.int32, (SUBLANES, lw), 0)

    def tile_step(c, carry):
        fr, fi, br, bi = carry
        cb = ntiles - 1 - c
        keep_f = jnp.where(start_ref[tile0 + c] != 0, 0.0, 1.0)
        keep_b = jnp.where(end_ref[tile0 + cb] != 0, 0.0, 1.0)
        fr, fi, br, bi = fr * keep_f, fi * keep_f, br * keep_b, bi * keep_b
        rf = pl.multiple_of(c * SUBLANES, SUBLANES)
        rbk = pl.multiple_of(cb * SUBLANES, SUBLANES)
        sfr = s_ref[pl.ds(rf, SUBLANES), 0:lw]
        sfi = s_ref[pl.ds(rf, SUBLANES), lw:2 * lw]
        sbr = s_ref[pl.ds(rbk, SUBLANES), 2 * lw:3 * lw]
        sbi = s_ref[pl.ds(rbk, SUBLANES), 3 * lw:4 * lw]
        xfr = xfi = xbr = xbi = jnp.zeros((SUBLANES, lw), F32)
        for j in range(SUBLANES):
            jb = SUBLANES - 1 - j
            xfr, xfi = jnp.where(row == j, fr, xfr), jnp.where(row == j, fi, xfi)
            xbr, xbi = jnp.where(row == jb, br, xbr), jnp.where(row == jb, bi, xbi)
            fr, fi = (afr * fr - afi * fi + sfr[j:j + 1], afr * fi + afi * fr + sfi[j:j + 1])
            br, bi = (abr * br - abi * bi + sbr[jb:jb + 1], abr * bi + abi * br + sbi[jb:jb + 1])
        x_ref[pl.ds(rf, SUBLANES), 0:lw] = xfr
        x_ref[pl.ds(rf, SUBLANES), lw:2 * lw] = xfi
        x_ref[pl.ds(rbk, SUBLANES), 2 * lw:3 * lw] = xbr
        x_ref[pl.ds(rbk, SUBLANES), 3 * lw:4 * lw] = xbi
        return fr, fi, br, bi

    zero = jnp.zeros((1, lw), F32)
    lax.fori_loop(0, ntiles, tile_step, (zero, zero, zero, zero))

    y = jnp.dot(u, m_ref[...], preferred_element_type=F32)
    y = y + jnp.dot(x_ref[...].astype(BF16), wo_ref[...], preferred_element_type=F32)
    for t in range(tc):
        yt = y[:, t * LANES:(t + 1) * LANES] + dsk_ref[...] * u_ref[t].astype(F32)
        o_ref[t] = jax.nn.gelu(yt, approximate=True).astype(o_ref.dtype)


def _s5_mixer(u3, mats, d_skip, seq_chunks):
    m, ws, wo, a = mats
    tc, nrows, width = u3.shape
    nt = m.shape[0]
    sw = ws.shape[2]
    rb = max(seq_chunks)
    assert nrows % rb == 0 and all(rb % c == 0 and c % SUBLANES == 0 for c in seq_chunks)
    bounds = np.cumsum([0] + list(seq_chunks))
    start = np.zeros(nrows // SUBLANES, np.int32)
    end = np.zeros(nrows // SUBLANES, np.int32)
    start[bounds[:-1] // SUBLANES] = 1
    end[bounds[1:] // SUBLANES - 1] = 1
    once = pl.Buffered(1)
    grid_spec = pltpu.PrefetchScalarGridSpec(
        num_scalar_prefetch=2,
        grid=(nt, nrows // rb),
        in_specs=[
            pl.BlockSpec((tc, rb, LANES), lambda g, r, st, en: (0, r, g)),
            pl.BlockSpec((None, tc * LANES, tc * LANES), lambda g, r, st, en: (g, 0, 0), pipeline_mode=once),
            pl.BlockSpec((None, tc * LANES, sw), lambda g, r, st, en: (g, 0, 0), pipeline_mode=once),
            pl.BlockSpec((None, sw, tc * LANES), lambda g, r, st, en: (g, 0, 0), pipeline_mode=once),
            pl.BlockSpec((4, sw // 4), lambda g, r, st, en: (0, g)),
            pl.BlockSpec((1, LANES), lambda g, r, st, en: (0, g)),
        ],
        out_specs=pl.BlockSpec((tc, rb, LANES), lambda g, r, st, en: (0, r, g)),
        scratch_shapes=[pltpu.VMEM((rb, sw), F32), pltpu.VMEM((rb, sw), F32)],
    )
    return pl.pallas_call(
        _s5_kernel,
        out_shape=jax.ShapeDtypeStruct((tc, nrows, width), BF16),
        grid_spec=grid_spec,
        compiler_params=_cparams(("arbitrary", "arbitrary")),
        name="s5_mixer",
    )(jnp.asarray(start), jnp.asarray(end), u3, m, ws, wo, a, d_skip.astype(F32).reshape(1, width))


def _mix_kernel(att_ref, y5_ref, ga_ref, gb_ref, wglu_ref, wpa_ref, wpb_ref, o_ref, glu_ref):
    @pl.when(pl.program_id(1) == 0)
    def _():
        gl = jnp.dot(y5_ref[...], wglu_ref[...], preferred_element_type=F32)
        sw = gl.shape[1] // 2
        glu_ref[...] = (gl[:, :sw] * _sigmoid(gl[:, sw:])).astype(glu_ref.dtype)

    ya = jnp.dot(att_ref[...], wpa_ref[...], preferred_element_type=F32)
    yb = jnp.dot(glu_ref[...], wpb_ref[...], preferred_element_type=F32)
    o = _sigmoid(ga_ref[...].astype(F32)) * ya + _sigmoid(gb_ref[...].astype(F32)) * yb
    o_ref[...] = o.astype(o_ref.dtype)


def _mix(att, y5, z, ga_col, gb_col, w_glu, w_pa, w_pb):
    t, na = att.shape
    sw = y5.shape[1]
    d = w_pa.shape[1]
    tm = _tile((1024, 512, 256, 128), t)
    tn = _tile((1024, 512, 256, 128), d, ga_col, gb_col)
    return pl.pallas_call(
        _mix_kernel,
        out_shape=jax.ShapeDtypeStruct((t, d), BF16),
        grid=(t // tm, d // tn),
        in_specs=[
            pl.BlockSpec((tm, na), lambda i, j: (i, 0)),
            pl.BlockSpec((tm, sw), lambda i, j: (i, 0)),
            pl.BlockSpec((tm, tn), lambda i, j: (i, ga_col // tn + j)),
            pl.BlockSpec((tm, tn), lambda i, j: (i, gb_col // tn + j)),
            pl.BlockSpec((sw, 2 * sw), lambda i, j: (0, 0)),
            pl.BlockSpec((na, tn), lambda i, j: (0, j)),
            pl.BlockSpec((sw, tn), lambda i, j: (0, j)),
        ],
        out_specs=pl.BlockSpec((tm, tn), lambda i, j: (i, j)),
        scratch_shapes=[pltpu.VMEM((tm, sw), BF16)],
        compiler_params=_cparams(("arbitrary", "arbitrary")),
        name="mixer_merge",
    )(att, y5, z, z, w_glu, w_pa, w_pb)


def _outproj_kernel(mix_ref, xa_ref, xb_ref, wo_ref, g2_ref, wr_ref, x2_ref, h2_ref, lg_ref, row_ref, *,
                    na_blocks):
    i, j = pl.program_id(0), pl.program_id(1)
    acc = jnp.dot(mix_ref[...], wo_ref[...], preferred_element_type=F32)

    @pl.when(i < na_blocks)
    def _():
        x2_ref[...] = xa_ref[...] + acc

    @pl.when(i >= na_blocks)
    def _():
        x2_ref[...] = xb_ref[...] + acc

    row_ref[j] = x2_ref[...]

    @pl.when(j == pl.num_programs(1) - 1)
    def _():
        x2 = jnp.concatenate([row_ref[jj] for jj in range(row_ref.shape[0])], axis=1)
        ms = jnp.mean(x2 * x2, axis=-1, keepdims=True)
        h2 = x2 * lax.rsqrt(ms + EPS) * g2_ref[...]
        hb = h2.astype(BF16)
        h2_ref[...] = hb
        hl = (h2 - hb.astype(F32)).astype(BF16)
        lg = jnp.dot(hb, wr_ref[0], preferred_element_type=F32)
        lg = lg + jnp.dot(hl, wr_ref[0], preferred_element_type=F32)
        lg = lg + jnp.dot(hb, wr_ref[1], preferred_element_type=F32)
        lg_ref[...] = lg


def _outproj(mixed, xa, xb, w_o, g2, w_router):
    t, d = mixed.shape
    tm = _tile((1024, 512, 256, 128), xa.shape[0], xb.shape[0])
    tn = _tile((512, 256, 128), d)
    na = xa.shape[0] // tm
    nr = w_router.shape[2]
    return pl.pallas_call(
        functools.partial(_outproj_kernel, na_blocks=na),
        out_shape=(jax.ShapeDtypeStruct((t, d), F32), jax.ShapeDtypeStruct((t, d), BF16),
                   jax.ShapeDtypeStruct((t, nr), F32)),
        grid=(t // tm, d // tn),
        in_specs=[
            pl.BlockSpec((tm, d), lambda i, j: (i, 0)),
            pl.BlockSpec((tm, tn), lambda i, j: (jnp.minimum(i, na - 1), jnp.where(i < na, j, d // tn - 1))),
            pl.BlockSpec((tm, tn), lambda i, j: (jnp.maximum(i - na, 0), jnp.where(i >= na, j, 0))),
            pl.BlockSpec((d, tn), lambda i, j: (0, j)),
            pl.BlockSpec((1, d), lambda i, j: (0, 0)),
            pl.BlockSpec((2, d, nr), lambda i, j: (0, 0, 0)),
        ],
        out_specs=(pl.BlockSpec((tm, tn), lambda i, j: (i, j)), pl.BlockSpec((tm, d), lambda i, j: (i, 0)),
                   pl.BlockSpec((tm, nr), lambda i, j: (i, 0))),
        scratch_shapes=[pltpu.VMEM((d // tn, tm, tn), F32)],
        compiler_params=_cparams(("arbitrary", "arbitrary")),
        name="outproj_norm_router",
    )(mixed, xa, xb, w_o, g2.reshape(1, d).astype(F32), w_router)


def _route(logits, b_rg, b_re, n_groups, n_experts, tm):
    t = logits.shape[0]
    epg = n_experts // n_groups
    gl = logits[:, :n_groups] + b_rg.astype(F32)
    g_prob = jax.nn.softmax(gl, axis=-1)
    g_idx = jnp.argmax(g_prob, axis=-1)
    g_w = jnp.max(g_prob, axis=-1)
    el = (logits[:, n_groups:n_groups + n_experts] + b_re.astype(F32)).reshape(t, n_groups, epg)
    g_oh = g_idx[:, None] == jnp.arange(n_groups)[None, :]
    e_sel = jnp.sum(jnp.where(g_oh[:, :, None], el, 0.0), axis=1)
    e_w, e_idx = lax.top_k(jax.nn.softmax(e_sel, axis=-1), TOP_K)
    gates = g_w[:, None] * (e_w / jnp.sum(e_w, axis=-1, keepdims=True))
    expert = (g_idx[:, None] * epg + e_idx).astype(jnp.int32)

    m = t * TOP_K
    eid = expert.reshape(m)
    oh = (eid[:, None] == jnp.arange(n_experts, dtype=jnp.int32)[None, :]).astype(jnp.int32)
    cum = jnp.cumsum(oh, axis=0)
    rank = jnp.sum(cum * oh, axis=1) - 1
    counts = cum[-1]
    padded = (counts + tm - 1) // tm * tm
    p_ends = jnp.cumsum(padded)
    p_starts = p_ends - padded
    dest = jnp.sum(oh * p_starts[None, :], axis=1) + rank
    n_blocks = m // tm + n_experts
    blk_start = jnp.arange(n_blocks, dtype=jnp.int32) * tm
    blk_eid = jnp.minimum(jnp.sum((p_ends[None, :] <= blk_start[:, None]).astype(jnp.int32), axis=1),
                          n_experts - 1).astype(jnp.int32)
    blk_valid = (blk_start < p_ends[-1]).astype(jnp.int32)
    tok = jnp.arange(m, dtype=jnp.int32) // TOP_K
    buf_tok = jnp.zeros((n_blocks * tm,), jnp.int32).at[dest].set(tok)
    return gates, dest.reshape(t, TOP_K), buf_tok, blk_eid, blk_valid


def _moe_kernel(eid_ref, valid_ref, x_ref, wg_ref, wu_ref, wd_ref, o_ref):
    i = pl.program_id(0)

    @pl.when(valid_ref[i] != 0)
    def _():
        x = x_ref[...]
        g = jnp.dot(x, wg_ref[...], preferred_element_type=F32)
        u = jnp.dot(x, wu_ref[...], preferred_element_type=F32)
        hid = (g * _sigmoid(g) * u).astype(BF16)
        o_ref[...] = jnp.dot(hid, wd_ref[...], preferred_element_type=F32).astype(o_ref.dtype)

    @pl.when(valid_ref[i] == 0)
    def _():
        o_ref[...] = jnp.zeros(o_ref.shape, o_ref.dtype)


def _moe_experts(x_buf, blk_eid, blk_valid, w_gate, w_up, w_down, tm):
    n, d = x_buf.shape
    ff = w_gate.shape[2]
    grid_spec = pltpu.PrefetchScalarGridSpec(
        num_scalar_prefetch=2,
        grid=(n // tm,),
        in_specs=[
            pl.BlockSpec((tm, d), lambda i, eid, valid: (i, 0)),
            pl.BlockSpec((None, d, ff), lambda i, eid, valid: (eid[i], 0, 0)),
            pl.BlockSpec((None, d, ff), lambda i, eid, valid: (eid[i], 0, 0)),
            pl.BlockSpec((None, ff, d), lambda i, eid, valid: (eid[i], 0, 0)),
        ],
        out_specs=pl.BlockSpec((tm, d), lambda i, eid, valid: (i, 0)),
    )
    return pl.pallas_call(
        _moe_kernel,
        out_shape=jax.ShapeDtypeStruct((n, d), BF16),
        grid_spec=grid_spec,
        compiler_params=_cparams(("arbitrary",)),
        name="moe_experts",
    )(blk_eid, blk_valid, x_buf, w_gate, w_up, w_down)


def _combine_kernel(x_ref, y0_ref, y1_ref, gt_ref, fg_ref, o_ref, *, final_norm):
    g = gt_ref[...]
    x = x_ref[...] + g[:, 0:1] * y0_ref[...].astype(F32) + g[:, 1:2] * y1_ref[...].astype(F32)
    if final_norm:
        ms = jnp.mean(x * x, axis=-1, keepdims=True)
        x = x * lax.rsqrt(ms + EPS) * fg_ref[...]
    o_ref[...] = x


def _combine(x2, y0, y1, gates, final_g, row0, nrows, final_norm):
    d = x2.shape[1]
    tm = _tile((512, 256, 128), row0, nrows) if row0 else _tile((512, 256, 128), nrows)
    b0 = row0 // tm
    return pl.pallas_call(
        functools.partial(_combine_kernel, final_norm=final_norm),
        out_shape=jax.ShapeDtypeStruct((nrows, d), F32),
        grid=(nrows // tm,),
        in_specs=[
            pl.BlockSpec((tm, d), lambda i: (b0 + i, 0)),
            pl.BlockSpec((tm, d), lambda i: (b0 + i, 0)),
            pl.BlockSpec((tm, d), lambda i: (b0 + i, 0)),
            pl.BlockSpec((tm, TOP_K), lambda i: (b0 + i, 0)),
            pl.BlockSpec((1, d), lambda i: (0, 0)),
        ],
        out_specs=pl.BlockSpec((tm, d), lambda i: (i, 0)),
        compiler_params=_cparams(("arbitrary",)),
        name="moe_combine",
    )(x2, y0, y1, gates, final_g.reshape(1, d).astype(F32))


def _layer(xa, xb, seq_a, seq_b, p, final_g, is_last):
    ta, d = xa.shape
    tb = xb.shape[0]
    t = ta + tb
    heads = p["rpb"].shape[0]
    na_w = heads * NA_HEAD_DIM
    _, g5, _, hg = p["b_re"].shape
    s5_w = g5 * hg
    tc = S5_CHUNK
    assert ta == seq_a, "the first stream is a single sequence"
    nb = tb // seq_b

    h = _prenorm(xa, xb, p["norm1_g"])
    z = _matmul(h, _to_bf16(p["w_in"]), BF16, "in_proj")

    att = _neighbourhood_attention(z, p["rpb"], ((1, seq_a // GRID_W), (nb, seq_b // GRID_W)), heads)

    u3 = z[:, 3 * na_w:3 * na_w + s5_w].reshape(t // tc, tc, s5_w).transpose(1, 0, 2)
    mats = _s5_matrices(p["lam_re"], p["lam_im"], p["log_step"], p["b_re"], p["b_im"], p["c_re"], p["c_im"], tc)
    y5 = _s5_mixer(u3, mats, p["d_skip"], [seq_a // tc] + [seq_b // tc] * nb)
    y5 = y5.transpose(1, 0, 2).reshape(t, s5_w)

    mixed = _mix(att, y5, z, 3 * na_w + s5_w, 3 * na_w + s5_w + d, p["w_glu"].astype(BF16),
                 _to_bf16(p["w_branch_a"]), p["w_branch_b"].astype(BF16))

    n_groups, n_experts = p["w_route_g"].shape[1], p["w_route_e"].shape[1]
    wr = jnp.concatenate([p["w_route_g"].astype(F32), p["w_route_e"].astype(F32),
                          jnp.zeros((d, ROUTER_LANES - n_groups - n_experts), F32)], axis=1)
    wr_hi = wr.astype(BF16)
    wr_lo = (wr - wr_hi.astype(F32)).astype(BF16)
    x2, h2, logits = _outproj(mixed, xa, xb, _to_bf16(p["w_out"]), p["norm2_g"], jnp.stack([wr_hi, wr_lo]))

    gates, dest, buf_tok, blk_eid, blk_valid = _route(logits, p["b_route_g"], p["b_route_e"], n_groups,
                                                     n_experts, MOE_TM)
    x_buf = jnp.take(h2, buf_tok, axis=0, mode="clip")
    y_buf = _moe_experts(x_buf, blk_eid, blk_valid, _to_bf16(p["w_gate"]), _to_bf16(p["w_up"]),
                         _to_bf16(p["w_down"]), MOE_TM)
    y0 = jnp.take(y_buf, dest[:, 0], axis=0, mode="clip")
    y1 = jnp.take(y_buf, dest[:, 1], axis=0, mode="clip")
    oa = _combine(x2, y0, y1, gates, final_g, 0, ta, is_last)
    ob = _combine(x2, y0, y1, gates, final_g, ta, tb, is_last)
    return oa, ob


_LAYER_PARAMS = ("norm1_g", "w_in", "rpb", "lam_re", "lam_im", "log_step", "b_re", "b_im", "c_re", "c_im",
                 "d_skip", "w_glu", "w_branch_a", "w_branch_b", "w_out", "norm2_g", "w_route_g", "b_route_g",
                 "w_route_e", "b_route_e", "w_gate", "w_up", "w_down")


def kernel(x_prompt, x_sample, norm1_g, w_in, rpb, lam_re, lam_im, log_step, b_re, b_im, c_re, c_im, d_skip,
           w_glu, w_branch_a, w_branch_b, w_out, norm2_g, w_route_g, b_route_g, w_route_e, b_route_e, w_gate,
           w_up, w_down, final_g):
    stacked = dict(zip(_LAYER_PARAMS, (norm1_g, w_in, rpb, lam_re, lam_im, log_step, b_re, b_im, c_re, c_im,
                                       d_skip, w_glu, w_branch_a, w_branch_b, w_out, norm2_g, w_route_g,
                                       b_route_g, w_route_e, b_route_e, w_gate, w_up, w_down)))
    depth = norm1_g.shape[0]
    ba, seq_a, d = x_prompt.shape
    bb, seq_b, _ = x_sample.shape
    xa = x_prompt.reshape(ba * seq_a, d)
    xb = x_sample.reshape(bb * seq_b, d)
    for l in range(depth):
        layer_p = {k: v[l] for k, v in stacked.items()}
        xa, xb = _layer(xa, xb, seq_a, seq_b, layer_p, final_g, l == depth - 1)
    return xa.reshape(ba, seq_a, d), xb.reshape(bb, seq_b, d)
```
